```python
import math
import jax, jax.numpy as jnp
from jax import lax
import numpy as np

D_MODEL = 1024
BATCH = 8
SEQ = 2048
DEPTH = 2

GRID_W = 64
CTX_LEN = 256
N_MOD = 9
D_FF = ((8 * D_MODEL + 3 * 256 - 1) // (3 * 256)) * 256
ADA_INIT = 0.5
LN_EPS = 1e-6
NEG_INF = -1e30

NA_HEAD_DIM = 64
NA_HEADS = (D_MODEL // 2) // NA_HEAD_DIM
NA_WIN_R = 8
NA_WIN_C = 16

GLA_HEADS = 4
GLA_DV = (D_MODEL // 2) // GLA_HEADS
GLA_DK = GLA_DV // 2
GLA_GATE_RANK = 16
GLA_TAU = 16.0
GLA_CHUNK = 64

DIFF_HEAD_DIM = 64
DIFF_HEADS = D_MODEL // (2 * DIFF_HEAD_DIM)
Q_BLOCK = 128
ROPE_THETA = 10000.0
ROPE_AXIS_DIM = DIFF_HEAD_DIM // 2

EVEN_SPLITS = (NA_HEADS * NA_HEAD_DIM,) * 3 + (GLA_HEADS * GLA_DK, GLA_HEADS * GLA_DK, GLA_HEADS * GLA_DV, GLA_HEADS * GLA_DV, GLA_GATE_RANK, GLA_GATE_RANK)
EVEN_IN = sum(EVEN_SPLITS)
EVEN_MIX = NA_HEADS * NA_HEAD_DIM + GLA_HEADS * GLA_DV
ODD_WIDTH = DIFF_HEADS * 2 * DIFF_HEAD_DIM
ODD_IN = 3 * ODD_WIDTH

kernel_name = 'hybrid_na_gla_diffattn_macaron_deepnorm'


def layer_norm(u):
    uf = u.astype(jnp.float32)
    mu = jnp.mean(uf, -1, keepdims=True)
    var = jnp.mean(jnp.square(uf - mu), -1, keepdims=True)
    return ((uf - mu) * lax.rsqrt(var + LN_EPS)).astype(u.dtype)


def rms_norm(u, g):
    uf = u.astype(jnp.float32)
    return (uf * lax.rsqrt(jnp.mean(uf * uf, -1, keepdims=True) + LN_EPS)).astype(u.dtype) * g


def modulate(u, shift, scale):
    return u * (1.0 + scale) + shift


def adaln_modulation(cond, w, b):
    m = jax.nn.silu(cond) @ w + b
    if m.ndim == 2:
        m = m[:, None, :]
    return jnp.split(m, N_MOD, axis=-1)


def swiglu(u, w_in, w_out):
    a, g = jnp.split(u @ w_in, 2, axis=-1)
    return (jax.nn.silu(a) * g) @ w_out


def split_cols(z, widths):
    idx = [int(i) for i in np.cumsum(widths)[:-1]]
    return jnp.split(z, idx, axis=-1)


def heads(t, n):
    return t.reshape(t.shape[0], t.shape[1], n, -1)


def softmax_attend(q, k, v):
    s = jnp.einsum('bthd,blhd->bhtl', q, k).astype(jnp.float32) * (q.shape[-1] ** -0.5)
    p = jax.nn.softmax(s, axis=-1).astype(v.dtype)
    return jnp.einsum('bhtl,blhd->bthd', p, v)


def neighbourhood_attention(q, k, v, kc, vc, rpb):
    B, S, H, dh = q.shape
    rows = S // GRID_W
    wr = min(NA_WIN_R, rows)
    grid = lambda t: t.reshape(B, rows, GRID_W, H, dh)
    qg, kg, vg = grid(q), grid(k), grid(v)
    r = jnp.arange(rows)
    row_start = jnp.clip(r - wr // 2, 0, rows - wr)
    row_idx = row_start[:, None] + jnp.arange(wr)[None, :]
    kb = kg[:, row_idx]
    vb = vg[:, row_idx]
    cidx = jnp.arange(GRID_W)
    col_start = jnp.clip(cidx - NA_WIN_C // 2, 0, GRID_W - NA_WIN_C)
    col_ok = (cidx[None, :] >= col_start[:, None]) & (cidx[None, :] < col_start[:, None] + NA_WIN_C)
    dr = row_idx - r[:, None] + NA_WIN_R - 1
    dc = jnp.clip(cidx[None, :] - cidx[:, None], -(NA_WIN_C - 1), NA_WIN_C - 1) + NA_WIN_C - 1
    bias = rpb[:, dr[:, None, :, None], dc[None, :, None, :]]
    scale = dh ** -0.5
    s_loc = jnp.einsum('brchd,briwhd->bhrciw', qg, kb).astype(jnp.float32) * scale + bias[None].astype(jnp.float32)
    s_loc = jnp.where(col_ok[:, None, :], s_loc, NEG_INF).reshape(B, H, rows, GRID_W, wr * GRID_W)
    s_ctx = jnp.einsum('brchd,blhd->bhrcl', qg, kc).astype(jnp.float32) * scale
    p = jax.nn.softmax(jnp.concatenate([s_loc, s_ctx], -1), axis=-1).astype(v.dtype)
    p_loc = p[..., :wr * GRID_W].reshape(B, H, rows, GRID_W, wr, GRID_W)
    p_ctx = p[..., wr * GRID_W:]
    o = jnp.einsum('bhrciw,briwhd->brchd', p_loc, vb) + jnp.einsum('bhrcl,blhd->brchd', p_ctx, vc)
    return o.reshape(B, S, H, dh)


def gla_chunked(q, k, v, log_g, state0, include_diag, with_output):
    B, T, H, dk = q.shape
    dv = v.shape[-1]
    n = T // GLA_CHUNK
    f32 = jnp.float32
    chunk = lambda t: t.astype(f32).reshape(B, n, GLA_CHUNK, H, t.shape[-1])
    qc, kc, vc, gc = chunk(q), chunk(k), chunk(v), chunk(log_g)
    b = jnp.cumsum(gc, axis=2)
    b_last = b[:, :, -1:]
    u = jnp.einsum('bnshd,bnshe->bnhde', kc * jnp.exp(b_last - b), vc)
    decay = jnp.exp(b_last[:, :, 0])

    def step(state, inp):
        dec, uc = inp
        return dec[..., None] * state + uc, state

    s_final, s_in = lax.scan(step, state0.astype(f32), (jnp.moveaxis(decay, 1, 0), jnp.moveaxis(u, 1, 0)))
    if not with_output:
        return None, s_final
    s_in = jnp.moveaxis(s_in, 0, 1)
    q_t = qc * jnp.exp(b)
    k_t = kc * jnp.exp(-b)
    mask = jnp.tril(jnp.ones((GLA_CHUNK, GLA_CHUNK), bool), 0 if include_diag else -1)
    att = jnp.where(mask, jnp.einsum('bnthd,bnshd->bnhts', q_t, k_t), 0.0)
    o = jnp.einsum('bnhts,bnshe->bnthe', att, vc) + jnp.einsum('bnthd,bnhde->bnthe', q_t, s_in)
    return o.reshape(B, T, H, dv).astype(v.dtype), s_final


def na_gla_mixer(xin, hin, w_in, w_out, rpb, w_gf, b_gf, w_gb, b_gb, norm_g, with_ctx_out):
    B, S, _ = xin.shape
    nq_x, nk_x, nv_x, gq_x, gk_x, gv_x, gr_x, zf_x, zb_x = split_cols(xin @ w_in, EVEN_SPLITS)
    nq_h, nk_h, nv_h, gq_h, gk_h, gv_h, gr_h, zf_h, zb_h = split_cols(hin @ w_in, EVEN_SPLITS)
    kh_na, vh_na = heads(nk_h, NA_HEADS), heads(nv_h, NA_HEADS)
    na_x = neighbourhood_attention(heads(nq_x, NA_HEADS), heads(nk_x, NA_HEADS), heads(nv_x, NA_HEADS), kh_na, vh_na, rpb)

    def gla_inputs(gq, gk, gv, zf, zb):
        q = heads(gq, GLA_HEADS) * (GLA_DK ** -0.5)
        lf = heads(jax.nn.log_sigmoid((zf @ w_gf + b_gf).astype(jnp.float32)) / GLA_TAU, GLA_HEADS)
        lb = heads(jax.nn.log_sigmoid((zb @ w_gb + b_gb).astype(jnp.float32)) / GLA_TAU, GLA_HEADS)
        return q, heads(gk, GLA_HEADS), heads(gv, GLA_HEADS), lf, lb

    flip = lambda t: jnp.flip(t, axis=1)
    s0 = jnp.zeros((B, GLA_HEADS, GLA_DK, GLA_DV), jnp.float32)
    qh, kh, vh, lfh, lbh = gla_inputs(gq_h, gk_h, gv_h, zf_h, zb_h)
    oh_f, sh_f = gla_chunked(qh, kh, vh, lfh, s0, True, with_ctx_out)
    oh_b, sh_b = gla_chunked(flip(qh), flip(kh), flip(vh), flip(lbh), s0, False, with_ctx_out)
    qx, kx, vx, lfx, lbx = gla_inputs(gq_x, gk_x, gv_x, zf_x, zb_x)
    ox_f, _ = gla_chunked(qx, kx, vx, lfx, sh_f, True, True)
    ox_b, _ = gla_chunked(flip(qx), flip(kx), flip(vx), flip(lbx), sh_b, False, True)

    def gla_out(o_f, o_b_flipped, gr):
        o = rms_norm(o_f + flip(o_b_flipped), norm_g) * jax.nn.silu(heads(gr, GLA_HEADS))
        return o.reshape(o.shape[0], o.shape[1], GLA_HEADS * GLA_DV)

    y_x = jnp.concatenate([na_x.reshape(B, S, -1), gla_out(ox_f, ox_b, gr_x)], -1) @ w_out
    y_h = None
    if with_ctx_out:
        na_h = softmax_attend(heads(nq_h, NA_HEADS), kh_na, vh_na)
        y_h = jnp.concatenate([na_h.reshape(B, hin.shape[1], -1), gla_out(oh_f, oh_b, gr_h)], -1) @ w_out
    return y_x, y_h


def axial_rope_tables(n_tokens):
    t = jnp.arange(n_tokens)
    row = (t // GRID_W).astype(jnp.float32)
    col = (t % GRID_W).astype(jnp.float32)
    inv = ROPE_THETA ** (-jnp.arange(0, ROPE_AXIS_DIM, 2, dtype=jnp.float32) / ROPE_AXIS_DIM)
    ar, ac = row[:, None] * inv, col[:, None] * inv
    ang = jnp.concatenate([ar, ar, ac, ac], -1)
    return jnp.cos(ang), jnp.sin(ang)


def apply_axial_rope(u, cos, sin):
    shape = (1, u.shape[1]) + (1,) * (u.ndim - 3) + (u.shape[-1],)
    cos = cos.reshape(shape).astype(u.dtype)
    sin = sin.reshape(shape).astype(u.dtype)
    half = u.shape[-1] // 2

    def rot(w):
        m = w.shape[-1] // 2
        return jnp.concatenate([-w[..., m:], w[..., :m]], -1)

    return u * cos + jnp.concatenate([rot(u[..., :half]), rot(u[..., half:])], -1) * sin


def differential_softmax_attend(q, k, v, lam):
    s = jnp.einsum('bqhmd,bkhmd->bhmqk', q, k).astype(jnp.float32) * (DIFF_HEAD_DIM ** -0.5)
    p = jax.nn.softmax(s, axis=-1)
    pd = (p[:, :, 0] - lam * p[:, :, 1]).astype(v.dtype)
    return jnp.einsum('bhqk,bkhe->bqhe', pd, v)


def diff_attention_mixer(xin, hin, w_in, w_out, lq1, lk1, lq2, lk2, subln_g, lambda_init, with_ctx_out):
    B, S, _ = xin.shape
    qk_heads = lambda t: t.reshape(t.shape[0], t.shape[1], DIFF_HEADS, 2, DIFF_HEAD_DIM)
    v_heads = lambda t: t.reshape(t.shape[0], t.shape[1], DIFF_HEADS, 2 * DIFF_HEAD_DIM)
    qx, kx, vx = jnp.split(xin @ w_in, 3, axis=-1)
    cos, sin = axial_rope_tables(S)
    qx = apply_axial_rope(qk_heads(qx), cos, sin)
    kx = apply_axial_rope(qk_heads(kx), cos, sin)
    vx = v_heads(vx)
    if with_ctx_out:
        qh, kh, vh = jnp.split(hin @ w_in, 3, axis=-1)
        qh = qk_heads(qh)
    else:
        kh, vh = jnp.split(hin @ w_in[:, ODD_WIDTH:], 2, axis=-1)
    kh, vh = qk_heads(kh), v_heads(vh)
    f32 = jnp.float32
    lam = (jnp.exp(jnp.sum(lq1.astype(f32) * lk1.astype(f32))) - jnp.exp(jnp.sum(lq2.astype(f32) * lk2.astype(f32))) + lambda_init)
    k_all = jnp.concatenate([kx, kh], axis=1)
    v_all = jnp.concatenate([vx, vh], axis=1)
    nb = S // Q_BLOCK
    q_blocks = jnp.moveaxis(qx.reshape(B, nb, Q_BLOCK, DIFF_HEADS, 2, DIFF_HEAD_DIM), 1, 0)
    o_blocks = lax.map(lambda qb: differential_softmax_attend(qb, k_all, v_all, lam), q_blocks)
    ox = jnp.moveaxis(o_blocks, 0, 1).reshape(B, S, DIFF_HEADS, 2 * DIFF_HEAD_DIM)

    def finish(o):
        o = rms_norm(o, subln_g) * (1.0 - lambda_init)
        return o.reshape(o.shape[0], o.shape[1], ODD_WIDTH) @ w_out

    y_x = finish(ox)
    y_h = finish(differential_softmax_attend(qh, kh, vh, lam)) if with_ctx_out else None
    return y_x, y_h


def setup_inputs(seed: int = 0) -> dict:
    key = jax.random.key(seed)
    ks = iter(jax.random.split(key, 64))
    f32 = jnp.float32
    beta = (8.0 * DEPTH) ** -0.25

    def normal(shape, scale):
        return jax.random.normal(next(ks), shape, f32) * scale

    inp = {}
    inp['x'] = normal((BATCH, SEQ, D_MODEL), 1.0)
    inp['c'] = normal((BATCH, D_MODEL), 1.0)
    inp['ctx'] = normal((BATCH, CTX_LEN, D_MODEL), 1.0)
    inp['c_ctx'] = normal((D_MODEL,), 1.0)
    for i in range(DEPTH):
        p = 'l%d_' % i
        inp[p + 'w_ada'] = normal((D_MODEL, N_MOD * D_MODEL), ADA_INIT * D_MODEL ** -0.5)
        inp[p + 'b_ada'] = normal((N_MOD * D_MODEL,), 0.02)
        inp[p + 'ffn1_w_in'] = normal((D_MODEL, 2 * D_FF), D_MODEL ** -0.5)
        inp[p + 'ffn1_w_out'] = normal((D_FF, D_MODEL), beta * D_FF ** -0.5)
        if i % 2 == 0:
            inp[p + 'mix_w_in'] = normal((D_MODEL, EVEN_IN), D_MODEL ** -0.5)
            inp[p + 'mix_w_out'] = normal((EVEN_MIX, D_MODEL), beta * EVEN_MIX ** -0.5)
            inp[p + 'na_rpb'] = normal((NA_HEADS, 2 * NA_WIN_R - 1, 2 * NA_WIN_C - 1), 0.05)
            inp[p + 'gla_w_gate_f'] = normal((GLA_GATE_RANK, GLA_HEADS * GLA_DK), GLA_GATE_RANK ** -0.5)
            inp[p + 'gla_b_gate_f'] = normal((GLA_HEADS * GLA_DK,), 0.1)
            inp[p + 'gla_w_gate_b'] = normal((GLA_GATE_RANK, GLA_HEADS * GLA_DK), GLA_GATE_RANK ** -0.5)
            inp[p + 'gla_b_gate_b'] = normal((GLA_HEADS * GLA_DK,), 0.1)
            inp[p + 'gla_norm_g'] = 1.0 + normal((GLA_DV,), 0.02)
        else:
            inp[p + 'mix_w_in'] = normal((D_MODEL, ODD_IN), D_MODEL ** -0.5)
            inp[p + 'mix_w_out'] = normal((ODD_WIDTH, D_MODEL), beta * ODD_WIDTH ** -0.5)
            inp[p + 'lambda_q1'] = normal((DIFF_HEAD_DIM,), 0.1)
            inp[p + 'lambda_k1'] = normal((DIFF_HEAD_DIM,), 0.1)
            inp[p + 'lambda_q2'] = normal((DIFF_HEAD_DIM,), 0.1)
            inp[p + 'lambda_k2'] = normal((DIFF_HEAD_DIM,), 0.1)
            inp[p + 'subln_g'] = 1.0 + normal((2 * DIFF_HEAD_DIM,), 0.02)
        inp[p + 'ffn2_w_in'] = normal((D_MODEL, 2 * D_FF), D_MODEL ** -0.5)
        inp[p + 'ffn2_w_out'] = normal((D_FF, D_MODEL), beta * D_FF ** -0.5)
    return inp


def reference(x, c, ctx, c_ctx,
              l0_w_ada, l0_b_ada, l0_ffn1_w_in, l0_ffn1_w_out, l0_mix_w_in, l0_mix_w_out, l0_na_rpb,
              l0_gla_w_gate_f, l0_gla_b_gate_f, l0_gla_w_gate_b, l0_gla_b_gate_b, l0_gla_norm_g,
              l0_ffn2_w_in, l0_ffn2_w_out,
              l1_w_ada, l1_b_ada, l1_ffn1_w_in, l1_ffn1_w_out, l1_mix_w_in, l1_mix_w_out,
              l1_lambda_q1, l1_lambda_k1, l1_lambda_q2, l1_lambda_k2, l1_subln_g,
              l1_ffn2_w_in, l1_ffn2_w_out):
    common = (
        (l0_w_ada, l0_b_ada, l0_ffn1_w_in, l0_ffn1_w_out, l0_ffn2_w_in, l0_ffn2_w_out),
        (l1_w_ada, l1_b_ada, l1_ffn1_w_in, l1_ffn1_w_out, l1_ffn2_w_in, l1_ffn2_w_out),
    )
    mixers = (
        (l0_mix_w_in, l0_mix_w_out, l0_na_rpb, l0_gla_w_gate_f, l0_gla_b_gate_f, l0_gla_w_gate_b, l0_gla_b_gate_b, l0_gla_norm_g),
        (l1_mix_w_in, l1_mix_w_out, l1_lambda_q1, l1_lambda_k1, l1_lambda_q2, l1_lambda_k2, l1_subln_g),
    )
    alpha = (2.0 * DEPTH) ** 0.25
    h = ctx
    for i in range(DEPTH):
        w_ada, b_ada, f1_in, f1_out, f2_in, f2_out = common[i]
        last = i == DEPTH - 1
        mx = adaln_modulation(c, w_ada, b_ada)
        mh = adaln_modulation(c_ctx, w_ada, b_ada)
        x = layer_norm(alpha * x + mx[2] * (0.5 * swiglu(modulate(x, mx[0], mx[1]), f1_in, f1_out)))
        h = layer_norm(alpha * h + mh[2] * (0.5 * swiglu(modulate(h, mh[0], mh[1]), f1_in, f1_out)))
        xin = modulate(x, mx[3], mx[4])
        hin = modulate(h, mh[3], mh[4])
        if i % 2 == 0:
            y_x, y_h = na_gla_mixer(xin, hin, *mixers[i], with_ctx_out=not last)
        else:
            lambda_init = 0.8 - 0.6 * math.exp(-0.3 * i)
            y_x, y_h = diff_attention_mixer(xin, hin, *mixers[i], lambda_init=lambda_init, with_ctx_out=not last)
        x = layer_norm(alpha * x + mx[5] * y_x)
        x = layer_norm(alpha * x + mx[8] * (0.5 * swiglu(modulate(x, mx[6], mx[7]), f2_in, f2_out)))
        if not last:
            h = layer_norm(alpha * h + mh[5] * y_h)
            h = layer_norm(alpha * h + mh[8] * (0.5 * swiglu(modulate(h, mh[6], mh[7]), f2_in, f2_out)))
    return x
```

```python
import functools
import math

import jax
import jax.numpy as jnp
from jax import lax
from jax.experimental import pallas as pl
from jax.experimental.pallas import tpu as pltpu

F32 = jnp.float32
BF16 = jnp.bfloat16

D_MODEL = 1024
DEPTH = 2
GRID_W = 64
N_MOD = 9
D_FF = 2816
LN_EPS = 1e-6
NEG_INF = -1e30
ALPHA = (2.0 * DEPTH) ** 0.25

NA_HEAD_DIM = 64
NA_HEADS = 8
NA_WIN_R = 8
NA_WIN_C = 16
NA_WIDTH = NA_HEADS * NA_HEAD_DIM

GLA_HEADS = 4
GLA_DV = 128
GLA_DK = 64
GLA_GATE_RANK = 16
GLA_TAU = 16.0
GLA_CHUNK = 64
GLA_QK = GLA_HEADS * GLA_DK
GLA_V = GLA_HEADS * GLA_DV

DIFF_HEAD_DIM = 64
DIFF_HEADS = 8
ROPE_THETA = 10000.0
ROPE_AXIS_DIM = DIFF_HEAD_DIM // 2
ODD_WIDTH = DIFF_HEADS * 2 * DIFF_HEAD_DIM

LANES = 128
SUBLANES = 8
VMEM_LIMIT_BYTES = 56 * 1024 * 1024

TOKEN_BLOCK = 512
FF_CHUNK = 256
DIFF_Q_BLOCK = 512


def _params(*sem):
    return pltpu.CompilerParams(dimension_semantics=sem, vmem_limit_bytes=VMEM_LIMIT_BYTES)


def _sigmoid(a):
    return 1.0 / (1.0 + jnp.exp(-a))


def _layer_norm(y):
    mu = jnp.mean(y, axis=-1, keepdims=True)
    yc = y - mu
    var = jnp.mean(yc * yc, axis=-1, keepdims=True)
    return yc * lax.rsqrt(var + LN_EPS)


def _mod_rows(mod_ref, row, k0):
    return tuple(mod_ref[pl.ds(row, 1), (k0 + t) * D_MODEL:(k0 + t + 1) * D_MODEL] for t in range(3))


def _dot(a, b):
    return jnp.dot(a, b, preferred_element_type=F32)


def _dot_nt(a, b):
    return lax.dot_general(a, b, (((1,), (1,)), ((), ())), preferred_element_type=F32)


def _ada_kernel(c_ref, w_ref, b_ref, o_ref):
    c = c_ref[...]
    s = (c * _sigmoid(c)).astype(BF16)
    o_ref[...] = _dot(s, w_ref[...].astype(BF16)) + b_ref[...]


def _ada(cond, w, b):
    rows = cond.shape[0]
    return pl.pallas_call(
        _ada_kernel,
        grid=(N_MOD,),
        in_specs=[
            pl.BlockSpec((rows, D_MODEL), lambda j: (0, 0)),
            pl.BlockSpec((D_MODEL, D_MODEL), lambda j: (0, j)),
            pl.BlockSpec((1, D_MODEL), lambda j: (0, j)),
        ],
        out_specs=pl.BlockSpec((rows, D_MODEL), lambda j: (0, j)),
        out_shape=jax.ShapeDtypeStruct((rows, N_MOD * D_MODEL), F32),
        compiler_params=_params("arbitrary"),
        name="ada_mod",
    )(cond, w, b.reshape(1, -1))


def _ffn_kernel(x_ref, mod_ref, win_ref, wout_ref, o_ref, xm_scr, h_scr, *, base, bpb, k0):
    row = base + pl.program_id(0) // bpb
    shift, scale, gate = _mod_rows(mod_ref, row, k0)
    x = x_ref[...]
    xm_scr[...] = (x * (1.0 + scale) + shift).astype(BF16)
    for j in range(D_FF // FF_CHUNK):
        lo = j * FF_CHUNK
        a = _dot(xm_scr[...], win_ref[:, lo:lo + FF_CHUNK])
        g = _dot(xm_scr[...], win_ref[:, D_FF + lo:D_FF + lo + FF_CHUNK])
        h_scr[:, lo:lo + FF_CHUNK] = (a * _sigmoid(a) * g).astype(BF16)
    y = _dot(h_scr[...], wout_ref[...])
    o_ref[...] = _layer_norm(ALPHA * x + gate * (0.5 * y))


def _resident(shape):
    return pl.BlockSpec(shape, lambda *_: (0,) * len(shape), pipeline_mode=pl.Buffered(1))


def _ffn(x, mod, w_in, w_out, *, base, bpb, k0):
    t = x.shape[0]
    tm = min(TOKEN_BLOCK, t)
    return pl.pallas_call(
        functools.partial(_ffn_kernel, base=base, bpb=bpb, k0=k0),
        grid=(t // tm,),
        in_specs=[
            pl.BlockSpec((tm, D_MODEL), lambda i: (i, 0)),
            _resident(mod.shape),
            _resident(w_in.shape),
            _resident(w_out.shape),
        ],
        out_specs=pl.BlockSpec((tm, D_MODEL), lambda i: (i, 0)),
        out_shape=jax.ShapeDtypeStruct((t, D_MODEL), F32),
        scratch_shapes=[pltpu.VMEM((tm, D_MODEL), BF16), pltpu.VMEM((tm, D_FF), BF16)],
        compiler_params=_params("arbitrary"),
        name="ffn",
    )(x, mod, w_in, w_out)


def _proj_even_kernel(x_ref, mod_ref, w_ref, wz_ref, wgf_ref, bgf_ref, wgb_ref, bgb_ref,
                      na_ref, gq_ref, gk_ref, gv_ref, gr_ref, lf_ref, lb_ref, xm_scr, *, base, bpb):
    row = base + pl.program_id(0) // bpb
    shift, scale, _ = _mod_rows(mod_ref, row, 3)
    xm_scr[...] = (x_ref[...] * (1.0 + scale) + shift).astype(BF16)

    def cols(lo, n):
        return _dot(xm_scr[...], w_ref[:, lo:lo + n])

    na_ref[:, 0:NA_WIDTH] = (cols(0, NA_WIDTH) * NA_HEAD_DIM ** -0.5).astype(BF16)
    na_ref[:, NA_WIDTH:2 * NA_WIDTH] = cols(NA_WIDTH, NA_WIDTH).astype(BF16)
    na_ref[:, 2 * NA_WIDTH:3 * NA_WIDTH] = cols(2 * NA_WIDTH, NA_WIDTH).astype(BF16)
    g0 = 3 * NA_WIDTH
    gq_ref[...] = (cols(g0, GLA_QK) * GLA_DK ** -0.5).astype(BF16)
    gk_ref[...] = cols(g0 + GLA_QK, GLA_QK).astype(BF16)
    gv_ref[...] = cols(g0 + 2 * GLA_QK, GLA_V).astype(BF16)
    gr_ref[...] = cols(g0 + 2 * GLA_QK + GLA_V, GLA_V).astype(BF16)
    z = _dot(xm_scr[...], wz_ref[...]).astype(BF16)
    for w_g, b_g, out in ((wgf_ref, bgf_ref, lf_ref), (wgb_ref, bgb_ref, lb_ref)):
        u = _dot(z, w_g[...]) + b_g[...]
        out[...] = (jnp.minimum(u, 0.0) - jnp.log(1.0 + jnp.exp(-jnp.abs(u)))) * (1.0 / GLA_TAU)


def _proj_even(x, mod, w_main, w_z, wgf, bgf, wgb, bgb, *, base, bpb):
    t = x.shape[0]
    tm = min(TOKEN_BLOCK, t)
    row = lambda n: pl.BlockSpec((tm, n), lambda i: (i, 0))
    outs = [(3 * NA_WIDTH, BF16), (GLA_QK, BF16), (GLA_QK, BF16), (GLA_V, BF16), (GLA_V, BF16), (GLA_QK, F32), (GLA_QK, F32)]
    return pl.pallas_call(
        functools.partial(_proj_even_kernel, base=base, bpb=bpb),
        grid=(t // tm,),
        in_specs=[row(D_MODEL), _resident(mod.shape), _resident(w_main.shape), _resident(w_z.shape),
                  _resident(wgf.shape), _resident(bgf.shape), _resident(wgb.shape), _resident(bgb.shape)],
        out_specs=[row(n) for n, _ in outs],
        out_shape=[jax.ShapeDtypeStruct((t, n), dt) for n, dt in outs],
        scratch_shapes=[pltpu.VMEM((tm, D_MODEL), BF16)],
        compiler_params=_params("arbitrary"),
        name="proj_even",
    )(x, mod, w_main, w_z, wgf, bgf, wgb, bgb)


NA_DR = 2 * NA_WIN_R - 1
NA_DC = 2 * NA_WIN_C - 1


def _na_bias_kernel(rpb_ref, o_ref):
    h = pl.program_id(0)
    c = lax.broadcasted_iota(jnp.int32, (GRID_W, LANES), 0)
    lane = lax.broadcasted_iota(jnp.int32, (GRID_W, LANES), 1)
    w = lane & (GRID_W - 1)
    second = lane >= GRID_W
    dc = jnp.clip(w - c, -(NA_WIN_C - 1), NA_WIN_C - 1) + NA_WIN_C - 1
    cs = jnp.clip(c - NA_WIN_C // 2, 0, GRID_W - NA_WIN_C)
    col_ok = (w >= cs) & (w < cs + NA_WIN_C)

    def per_d(d, carry):
        def per_k(k, acc):
            v0 = rpb_ref[(h * NA_DR + d) * NA_DC + k]
            v1 = rpb_ref[(h * NA_DR + d + 1) * NA_DC + k]
            return jnp.where(dc == k, jnp.where(second, v1, v0), acc)

        acc = lax.fori_loop(0, NA_DC, per_k, jnp.zeros((GRID_W, LANES), F32))
        o_ref[0, d] = jnp.where(col_ok, acc, NEG_INF)
        return carry

    lax.fori_loop(0, NA_DR - 1, per_d, 0)


def _na_bias(rpb):
    return pl.pallas_call(
        _na_bias_kernel,
        grid=(NA_HEADS,),
        in_specs=[pl.BlockSpec(memory_space=pltpu.SMEM)],
        out_specs=pl.BlockSpec((1, NA_DR - 1, GRID_W, LANES), lambda h: (h, 0, 0, 0)),
        out_shape=jax.ShapeDtypeStruct((NA_HEADS, NA_DR - 1, GRID_W, LANES), F32),
        compiler_params=_params("arbitrary"),
        name="na_bias",
    )(rpb.reshape(-1))


def _na_kernel(q_ref, k_ref, v_ref, kc_ref, vc_ref, tb_ref, o_ref, *, rows):
    r = pl.program_id(1)
    rs = jnp.clip(r - NA_WIN_R // 2, 0, rows - NA_WIN_R)
    off = rs - r + NA_WIN_R - 1
    k_lo = pl.multiple_of(rs * GRID_W, GRID_W)
    n_win = NA_WIN_R * GRID_W
    lane = lax.broadcasted_iota(jnp.int32, (1, LANES), 1)
    for p in range(NA_HEADS // 2):
        sl = slice(p * LANES, (p + 1) * LANES)
        q2 = q_ref[:, sl]
        k2 = k_ref[pl.ds(k_lo, n_win), sl]
        v2 = v_ref[pl.ds(k_lo, n_win), sl]
        kc2 = kc_ref[:, sl]
        vc2 = vc_ref[:, sl]
        outs = []
        for hh in range(2):
            head = 2 * p + hh
            mine = (lane >= hh * NA_HEAD_DIM) & (lane < (hh + 1) * NA_HEAD_DIM)
            qm = jnp.where(mine, q2, jnp.zeros_like(q2))
            bias = jnp.concatenate([tb_ref[head, off + 2 * j] for j in range(NA_WIN_R // 2)], axis=1)
            s_loc = _dot_nt(qm, k2) + bias
            s_ctx = _dot_nt(qm, kc2)
            m = jnp.maximum(jnp.max(s_loc, axis=-1, keepdims=True), jnp.max(s_ctx, axis=-1, keepdims=True))
            p_loc = jnp.exp(s_loc - m)
            p_ctx = jnp.exp(s_ctx - m)
            den = jnp.sum(p_loc, axis=-1, keepdims=True) + jnp.sum(p_ctx, axis=-1, keepdims=True)
            o2 = _dot(p_loc.astype(BF16), v2) + _dot(p_ctx.astype(BF16), vc2)
            outs.append(o2 / den)
        o_ref[:, sl] = jnp.where(lane < NA_HEAD_DIM, outs[0], outs[1]).astype(o_ref.dtype)


def _na(qkv_x, qkv_h, tb, *, batch, seq, ctx_len):
    rows = seq // GRID_W
    return pl.pallas_call(
        functools.partial(_na_kernel, rows=rows),
        grid=(batch, rows),
        in_specs=[
            pl.BlockSpec((GRID_W, NA_WIDTH), lambda b, r: (b * rows + r, 0)),
            pl.BlockSpec((seq, NA_WIDTH), lambda b, r: (b, 1)),
            pl.BlockSpec((seq, NA_WIDTH), lambda b, r: (b, 2)),
            pl.BlockSpec((ctx_len, NA_WIDTH), lambda b, r: (b, 1)),
            pl.BlockSpec((ctx_len, NA_WIDTH), lambda b, r: (b, 2)),
            pl.BlockSpec(tb.shape, lambda b, r: (0, 0, 0, 0)),
        ],
        out_specs=pl.BlockSpec((GRID_W, NA_WIDTH), lambda b, r: (b * rows + r, 0)),
        out_shape=jax.ShapeDtypeStruct((batch * seq, NA_WIDTH), BF16),
        compiler_params=_params("arbitrary", "arbitrary"),
        name="na_attn",
    )(qkv_x, qkv_x, qkv_x, qkv_h, qkv_h, tb)


def _na_ctx_kernel(q_ref, k_ref, v_ref, o_ref):
    lane = lax.broadcasted_iota(jnp.int32, (1, LANES), 1)
    for p in range(NA_HEADS // 2):
        sl = slice(p * LANES, (p + 1) * LANES)
        q2, k2, v2 = q_ref[:, sl], k_ref[:, sl], v_ref[:, sl]
        outs = []
        for hh in range(2):
            mine = (lane >= hh * NA_HEAD_DIM) & (lane < (hh + 1) * NA_HEAD_DIM)
            s = _dot_nt(jnp.where(mine, q2, jnp.zeros_like(q2)), k2)
            e = jnp.exp(s - jnp.max(s, axis=-1, keepdims=True))
            outs.append(_dot(e.astype(BF16), v2) / jnp.sum(e, axis=-1, keepdims=True))
        o_ref[:, sl] = jnp.where(lane < NA_HEAD_DIM, outs[0], outs[1]).astype(o_ref.dtype)


def _na_ctx(qkv_h, *, batch, ctx_len):
    spec = lambda j: pl.BlockSpec((ctx_len, NA_WIDTH), lambda b: (b, j))
    return pl.pallas_call(
        _na_ctx_kernel,
        grid=(batch,),
        in_specs=[spec(0), spec(1), spec(2)],
        out_specs=spec(0),
        out_shape=jax.ShapeDtypeStruct((batch * ctx_len, NA_WIDTH), BF16),
        compiler_params=_params("arbitrary"),
        name="na_ctx",
    )(qkv_h, qkv_h, qkv_h)


def _gla_chunk(q, k, v, g, st_ref, *, forward):
    c = GLA_CHUNK
    ti = lax.broadcasted_iota(jnp.int32, (c, c), 0)
    si = lax.broadcasted_iota(jnp.int32, (c, c), 1)
    tri = jnp.where((si <= ti) if forward else (si >= ti), 1.0, 0.0).astype(F32)
    b = jnp.dot(tri, g, preferred_element_type=F32, precision=lax.Precision.HIGHEST)
    tot = b[c - 1:c, :] if forward else b[0:1, :]
    qt = (q * jnp.exp(b)).astype(BF16)
    kt = k * jnp.exp(-b)
    kd = (k * jnp.exp(tot - b)).astype(BF16)
    lane = lax.broadcasted_iota(jnp.int32, (c, 2 * GLA_DK), 1)
    first = lane < GLA_DK
    kbd = jnp.concatenate([jnp.where(first, kt, 0.0), jnp.where(first, 0.0, kt)], axis=0).astype(BF16)
    att = _dot_nt(qt, kbd)
    trow = lax.broadcasted_iota(jnp.int32, (c, 2 * GLA_DK), 0)
    src = lane & (c - 1)
    keep = (src <= trow) if forward else (src > trow)
    att = jnp.where(keep, att, 0.0).astype(BF16)
    vlane = lax.broadcasted_iota(jnp.int32, (c, 2 * GLA_DV), 1)
    vfirst = vlane < GLA_DV
    vbd = jnp.concatenate([jnp.where(vfirst, v, 0.0), jnp.where(vfirst, 0.0, v)], axis=0).astype(BF16)
    st = st_ref[...]
    o = _dot(att, vbd) + _dot_nt(qt, st.astype(BF16))
    ut = _dot(v.T.astype(BF16), kd)
    urow = lax.broadcasted_iota(jnp.int32, (2 * GLA_DV, 2 * GLA_DK), 0)
    ucol = lax.broadcasted_iota(jnp.int32, (2 * GLA_DV, 2 * GLA_DK), 1)
    same_head = (urow < GLA_DV) == (ucol < GLA_DK)
    st_ref[...] = st * jnp.exp(tot) + jnp.where(same_head, ut, 0.0)
    return o


def _gla_kernel(hq_ref, hk_ref, hv_ref, hlf_ref, hlb_ref, xq_ref, xk_ref, xv_ref, xlf_ref, xlb_ref, ng_ref,
                oh_ref, ox_ref, sf_scr, sb_scr, hf_scr, hb_scr, xf_scr, xb_scr):
    c = GLA_CHUNK
    sf_scr[...] = jnp.zeros_like(sf_scr)
    sb_scr[...] = jnp.zeros_like(sb_scr)

    def run(q_ref, k_ref, v_ref, lf_ref, lb_ref, of_scr, ob_scr):
        n = q_ref.shape[0] // c

        def body(i, carry):
            for forward, lg_ref, st_ref, o_scr, idx in ((True, lf_ref, sf_scr, of_scr, i), (False, lb_ref, sb_scr, ob_scr, n - 1 - i)):
                rows = pl.ds(pl.multiple_of(idx * c, c), c)
                o_scr[rows, :] = _gla_chunk(q_ref[rows, :].astype(F32), k_ref[rows, :].astype(F32), v_ref[rows, :].astype(F32),
                                            lg_ref[rows, :], st_ref, forward=forward)
            return carry

        lax.fori_loop(0, n, body, 0)

    run(hq_ref, hk_ref, hv_ref, hlf_ref, hlb_ref, hf_scr, hb_scr)
    run(xq_ref, xk_ref, xv_ref, xlf_ref, xlb_ref, xf_scr, xb_scr)

    ng = ng_ref[...]
    for of_scr, ob_scr, out in ((hf_scr, hb_scr, oh_ref), (xf_scr, xb_scr, ox_ref)):
        for hh in range(2):
            sl = slice(hh * GLA_DV, (hh + 1) * GLA_DV)
            o = of_scr[:, sl] + ob_scr[:, sl]
            out[:, sl] = (o * lax.rsqrt(jnp.mean(o * o, axis=-1, keepdims=True) + LN_EPS) * ng).astype(out.dtype)


def _gla(h_parts, x_parts, norm_g, *, batch, seq, ctx_len):
    def specs(n):
        qk = pl.BlockSpec((n, 2 * GLA_DK), lambda b, p: (b, p))
        vv = pl.BlockSpec((n, 2 * GLA_DV), lambda b, p: (b, p))
        return [qk, qk, vv, qk, qk], vv

    h_in, h_out = specs(ctx_len)
    x_in, x_out = specs(seq)
    st = pltpu.VMEM((2 * GLA_DV, 2 * GLA_DK), F32)
    acc = lambda n: pltpu.VMEM((n, 2 * GLA_DV), F32)
    return pl.pallas_call(
        _gla_kernel,
        grid=(batch, GLA_HEADS // 2),
        in_specs=h_in + x_in + [pl.BlockSpec((1, GLA_DV), lambda b, p: (0, 0))],
        out_specs=[h_out, x_out],
        out_shape=[jax.ShapeDtypeStruct((batch * ctx_len, GLA_V), BF16), jax.ShapeDtypeStruct((batch * seq, GLA_V), BF16)],
        scratch_shapes=[st, st, acc(ctx_len), acc(ctx_len), acc(seq), acc(seq)],
        compiler_params=_params("arbitrary", "arbitrary"),
        name="gla",
    )(*h_parts, *x_parts, norm_g.reshape(1, GLA_DV))


def _out_even_kernel(na_ref, gl_ref, gr_ref, x_ref, mod_ref, w_ref, o_ref, *, base, bpb):
    row = base + pl.program_id(0) // bpb
    gate = mod_ref[pl.ds(row, 1), 5 * D_MODEL:6 * D_MODEL]
    gr = gr_ref[...].astype(F32)
    gl = (gl_ref[...].astype(F32) * (gr * _sigmoid(gr))).astype(BF16)
    y = _dot(na_ref[...], w_ref[0:NA_WIDTH, :]) + _dot(gl, w_ref[NA_WIDTH:, :])
    o_ref[...] = _layer_norm(ALPHA * x_ref[...] + gate * y)


def _out_even(na_o, gl_o, gr, x, mod, w, *, base, bpb):
    t = x.shape[0]
    tm = min(TOKEN_BLOCK, t)
    row = lambda n: pl.BlockSpec((tm, n), lambda i: (i, 0))
    return pl.pallas_call(
        functools.partial(_out_even_kernel, base=base, bpb=bpb),
        grid=(t // tm,),
        in_specs=[row(NA_WIDTH), row(GLA_V), row(GLA_V), row(D_MODEL), _resident(mod.shape), _resident(w.shape)],
        out_specs=row(D_MODEL),
        out_shape=jax.ShapeDtypeStruct((t, D_MODEL), F32),
        compiler_params=_params("arbitrary"),
        name="out_even",
    )(na_o, gl_o, gr, x, mod, w)


def _out_odd_kernel(a_ref, x_ref, mod_ref, w_ref, o_ref, *, base, bpb):
    row = base + pl.program_id(0) // bpb
    gate = mod_ref[pl.ds(row, 1), 5 * D_MODEL:6 * D_MODEL]
    o_ref[...] = _layer_norm(ALPHA * x_ref[...] + gate * _dot(a_ref[...], w_ref[...]))


def _out_odd(a, x, mod, w, *, base, bpb):
    t = x.shape[0]
    tm = min(TOKEN_BLOCK, t)
    row = lambda n: pl.BlockSpec((tm, n), lambda i: (i, 0))
    return pl.pallas_call(
        functools.partial(_out_odd_kernel, base=base, bpb=bpb),
        grid=(t // tm,),
        in_specs=[row(ODD_WIDTH), row(D_MODEL), _resident(mod.shape), _resident(w.shape)],
        out_specs=row(D_MODEL),
        out_shape=jax.ShapeDtypeStruct((t, D_MODEL), F32),
        compiler_params=_params("arbitrary"),
        name="out_odd",
    )(a, x, mod, w)


ROPE_SHIFT = ROPE_AXIS_DIM // 2
PROJ_COLS = 512


def _proj_odd_kernel(*refs, base, bpb, plan):
    x_ref, mod_ref, w_ref, cos_ref, sa_ref, sb_ref = refs[:6]
    outs = refs[6:6 + len(plan)]
    xm_scr = refs[-1]
    row = base + pl.program_id(0) // bpb
    shift, scale, _ = _mod_rows(mod_ref, row, 3)
    xm_scr[...] = (x_ref[...] * (1.0 + scale) + shift).astype(BF16)
    for (lo, rope, mul), out in zip(plan, outs):
        for c0 in range(0, ODD_WIDTH, PROJ_COLS):
            z = _dot(xm_scr[...], w_ref[:, lo + c0:lo + c0 + PROJ_COLS])
            for j in range(PROJ_COLS // LANES):
                u = z[:, j * LANES:(j + 1) * LANES]
                if rope:
                    u = (u * cos_ref[...] + pltpu.roll(u, LANES - ROPE_SHIFT, 1) * sa_ref[...]
                         + pltpu.roll(u, ROPE_SHIFT, 1) * sb_ref[...])
                if mul != 1.0:
                    u = u * mul
                out[:, c0 + j * LANES:c0 + (j + 1) * LANES] = u.astype(out.dtype)


def _proj_odd(x, mod, w, tables, *, base, bpb, plan, pos_blocks):
    t = x.shape[0]
    tm = min(TOKEN_BLOCK, t)
    row = lambda n: pl.BlockSpec((tm, n), lambda i: (i, 0))
    tab = pl.BlockSpec((tm, LANES), lambda i: (i % pos_blocks, 0))
    return pl.pallas_call(
        functools.partial(_proj_odd_kernel, base=base, bpb=bpb, plan=plan),
        grid=(t // tm,),
        in_specs=[row(D_MODEL), _resident(mod.shape), _resident(w.shape), tab, tab, tab],
        out_specs=[row(ODD_WIDTH) for _ in plan],
        out_shape=[jax.ShapeDtypeStruct((t, ODD_WIDTH), BF16) for _ in plan],
        scratch_shapes=[pltpu.VMEM((tm, D_MODEL), BF16)],
        compiler_params=_params("arbitrary"),
        name="proj_odd",
    )(x, mod, w, *tables)


def _rope_tables(seq):
    t = jnp.arange(seq)
    rowp = (t // GRID_W).astype(F32)
    colp = (t % GRID_W).astype(F32)
    inv = ROPE_THETA ** (-jnp.arange(0, ROPE_AXIS_DIM, 2, dtype=F32) / ROPE_AXIS_DIM)
    ar, ac = rowp[:, None] * inv, colp[:, None] * inv
    ang = jnp.concatenate([ar, ar, ac, ac], -1)
    cos, sin = jnp.cos(ang), jnp.sin(ang)
    low = (jnp.arange(DIFF_HEAD_DIM) % ROPE_AXIS_DIM) < ROPE_SHIFT
    sa = jnp.where(low, -sin, 0.0)
    sb = jnp.where(low, 0.0, sin)
    two = lambda a: jnp.concatenate([a, a], -1)
    return two(cos), two(sa), two(sb)


def _diff_kernel(q_ref, kx_ref, vx_ref, kh_ref, vh_ref, lam_ref, g_ref, o_ref, *, lambda_init):
    lp = lam_ref[...]
    lam = (jnp.exp(jnp.sum(lp[0:1] * lp[1:2], axis=-1, keepdims=True))
           - jnp.exp(jnp.sum(lp[2:3] * lp[3:4], axis=-1, keepdims=True)) + lambda_init)
    q = q_ref[...]
    lane = lax.broadcasted_iota(jnp.int32, (1, LANES), 1)
    parts = []
    for m in range(2):
        mine = (lane >= m * DIFF_HEAD_DIM) & (lane < (m + 1) * DIFF_HEAD_DIM)
        qm = jnp.where(mine, q, jnp.zeros_like(q))
        sx = _dot_nt(qm, kx_ref[...])
        sh = _dot_nt(qm, kh_ref[...])
        mx = jnp.maximum(jnp.max(sx, axis=-1, keepdims=True), jnp.max(sh, axis=-1, keepdims=True))
        px = jnp.exp(sx - mx)
        ph = jnp.exp(sh - mx)
        den = jnp.sum(px, axis=-1, keepdims=True) + jnp.sum(ph, axis=-1, keepdims=True)
        parts.append((px, ph, den))
    c1 = 1.0 / parts[0][2]
    c2 = lam / parts[1][2]
    pdx = (parts[0][0] * c1 - parts[1][0] * c2).astype(BF16)
    pdh = (parts[0][1] * c1 - parts[1][1] * c2).astype(BF16)
    o = _dot(pdx, vx_ref[...]) + _dot(pdh, vh_ref[...])
    o = o * lax.rsqrt(jnp.mean(o * o, axis=-1, keepdims=True) + LN_EPS) * g_ref[...] * (1.0 - lambda_init)
    o_ref[...] = o.astype(o_ref.dtype)


def _diff_attn(q, kx, vx, kh, vh, lam_rows, subln_g, *, batch, seq, ctx_len, lambda_init):
    tq = min(DIFF_Q_BLOCK, seq)
    nq = seq // tq
    head = 2 * DIFF_HEAD_DIM
    kv = lambda n: pl.BlockSpec((n, head), lambda b, h, i: (b, h))
    return pl.pallas_call(
        functools.partial(_diff_kernel, lambda_init=lambda_init),
        grid=(batch, DIFF_HEADS, nq),
        in_specs=[
            pl.BlockSpec((tq, head), lambda b, h, i: (b * nq + i, h)),
            kv(seq), kv(seq), kv(ctx_len), kv(ctx_len),
            pl.BlockSpec(lam_rows.shape, lambda b, h, i: (0, 0)),
            pl.BlockSpec((1, head), lambda b, h, i: (0, 0)),
        ],
        out_specs=pl.BlockSpec((tq, head), lambda b, h, i: (b * nq + i, h)),
        out_shape=jax.ShapeDtypeStruct((batch * seq, ODD_WIDTH), BF16),
        compiler_params=_params("arbitrary", "arbitrary", "arbitrary"),
        name="diff_attn",
    )(q, kx, vx, kh, vh, lam_rows, subln_g.reshape(1, head))


def kernel(x, c, ctx, c_ctx, l0_w_ada, l0_b_ada, l0_ffn1_w_in, l0_ffn1_w_out, l0_mix_w_in, l0_mix_w_out, l0_na_rpb, l0_gla_w_gate_f, l0_gla_b_gate_f, l0_gla_w_gate_b, l0_gla_b_gate_b, l0_gla_norm_g, l0_ffn2_w_in, l0_ffn2_w_out, l1_w_ada, l1_b_ada, l1_ffn1_w_in, l1_ffn1_w_out, l1_mix_w_in, l1_mix_w_out, l1_lambda_q1, l1_lambda_k1, l1_lambda_q2, l1_lambda_k2, l1_subln_g, l1_ffn2_w_in, l1_ffn2_w_out):
    batch, seq, _ = x.shape
    ctx_len = ctx.shape[1]
    assert seq % TOKEN_BLOCK == 0 and seq % GRID_W == 0 and seq // GRID_W >= NA_WIN_R
    assert ctx_len % GLA_CHUNK == 0
    bf = lambda a: a.astype(BF16)

    n_cond = -(-(batch + 1) // SUBLANES) * SUBLANES
    cond = jnp.zeros((n_cond, D_MODEL), F32).at[:batch].set(c).at[batch].set(c_ctx)
    mod0 = _ada(cond, l0_w_ada, l0_b_ada)
    mod1 = _ada(cond, l1_w_ada, l1_b_ada)

    xt = x.reshape(batch * seq, D_MODEL)
    ht = ctx.reshape(batch * ctx_len, D_MODEL)
    x_blocks = dict(base=0, bpb=seq // TOKEN_BLOCK)
    h_blocks = dict(base=batch, bpb=batch * ctx_len)

    w_in, w_out = bf(l0_ffn1_w_in), bf(l0_ffn1_w_out)
    xt = _ffn(xt, mod0, w_in, w_out, k0=0, **x_blocks)
    ht = _ffn(ht, mod0, w_in, w_out, k0=0, **h_blocks)

    n_main = 3 * NA_WIDTH + 2 * GLA_QK + 2 * GLA_V
    w_main = bf(l0_mix_w_in[:, :n_main])
    w_z = bf(jnp.pad(l0_mix_w_in[:, n_main:], ((0, 0), (0, LANES - 2 * GLA_GATE_RANK))))
    wgf = bf(jnp.pad(l0_gla_w_gate_f, ((0, LANES - GLA_GATE_RANK), (0, 0))))
    wgb = bf(jnp.pad(l0_gla_w_gate_b, ((GLA_GATE_RANK, LANES - 2 * GLA_GATE_RANK), (0, 0))))
    gate_args = (w_main, w_z, wgf, l0_gla_b_gate_f.reshape(1, -1), wgb, l0_gla_b_gate_b.reshape(1, -1))
    na_x, gq_x, gk_x, gv_x, gr_x, lf_x, lb_x = _proj_even(xt, mod0, *gate_args, **x_blocks)
    na_h, gq_h, gk_h, gv_h, gr_h, lf_h, lb_h = _proj_even(ht, mod0, *gate_args, **h_blocks)

    tb = _na_bias(l0_na_rpb)
    nao_x = _na(na_x, na_h, tb, batch=batch, seq=seq, ctx_len=ctx_len)
    nao_h = _na_ctx(na_h, batch=batch, ctx_len=ctx_len)
    glo_h, glo_x = _gla((gq_h, gk_h, gv_h, lf_h, lb_h), (gq_x, gk_x, gv_x, lf_x, lb_x), l0_gla_norm_g,
                        batch=batch, seq=seq, ctx_len=ctx_len)
    w_mix_out = bf(l0_mix_w_out)
    xt = _out_even(nao_x, glo_x, gr_x, xt, mod0, w_mix_out, **x_blocks)
    ht = _out_even(nao_h, glo_h, gr_h, ht, mod0, w_mix_out, **h_blocks)

    w_in, w_out = bf(l0_ffn2_w_in), bf(l0_ffn2_w_out)
    xt = _ffn(xt, mod0, w_in, w_out, k0=6, **x_blocks)
    ht = _ffn(ht, mod0, w_in, w_out, k0=6, **h_blocks)

    w_in, w_out = bf(l1_ffn1_w_in), bf(l1_ffn1_w_out)
    xt = _ffn(xt, mod1, w_in, w_out, k0=0, **x_blocks)
    ht = _ffn(ht, mod1, w_in, w_out, k0=0, **h_blocks)

    lambda_init = 0.8 - 0.6 * math.exp(-0.3 * 1)
    tables = _rope_tables(seq)
    w_qkv = bf(l1_mix_w_in)
    scale = DIFF_HEAD_DIM ** -0.5
    q, kx, vx = _proj_odd(xt, mod1, w_qkv, tables, plan=((0, True, scale), (ODD_WIDTH, True, 1.0), (2 * ODD_WIDTH, False, 1.0)),
                          pos_blocks=seq // TOKEN_BLOCK, **x_blocks)
    kh, vh = _proj_odd(ht, mod1, w_qkv, tables, plan=((ODD_WIDTH, False, 1.0), (2 * ODD_WIDTH, False, 1.0)),
                       pos_blocks=1, **h_blocks)
    lam_rows = jnp.zeros((SUBLANES, 2 * DIFF_HEAD_DIM), F32)
    for i, v in enumerate((l1_lambda_q1, l1_lambda_k1, l1_lambda_q2, l1_lambda_k2)):
        lam_rows = lam_rows.at[i, :DIFF_HEAD_DIM].set(v)
    att = _diff_attn(q, kx, vx, kh, vh, lam_rows, l1_subln_g, batch=batch, seq=seq, ctx_len=ctx_len, lambda_init=lambda_init)
    xt = _out_odd(att, xt, mod1, bf(l1_mix_w_out), **x_blocks)

    xt = _ffn(xt, mod1, bf(l1_ffn2_w_in), bf(l1_ffn2_w_out), k0=6, **x_blocks)
    return xt.reshape(batch, seq, D_MODEL)
```
